```python
import math
import jax, jax.numpy as jnp
from jax import lax
import numpy as np

D_MODEL = 1024
BATCH = 8
SEQ = 4096
DEPTH = 4

CHUNK = 64
Q_BLOCK = 128
EPS = 1e-6
MIX_WIDTH = D_MODEL
ATT_WIDTH = MIX_WIDTH // 2
CONV_WIDTH = MIX_WIDTH // 4
SSM_WIDTH = MIX_WIDTH - ATT_WIDTH - CONV_WIDTH
ATT_HEAD_DIM = 64
ATT_V_DIM = 2 * ATT_HEAD_DIM
N_ATT_HEADS = ATT_WIDTH // ATT_V_DIM
QK_WIDTH = N_ATT_HEADS * 2 * ATT_HEAD_DIM
CONV_KERNEL = 31
SSM_GROUP = 16
N_SSM_GROUPS = SSM_WIDTH // SSM_GROUP
SSM_STATE = 64
D_FF = 4 * D_MODEL
PLE_DIM = 256
IN_WIDTH = 2 * QK_WIDTH + ATT_WIDTH + 2 * CONV_WIDTH + SSM_WIDTH
SPLITS = (QK_WIDTH, 2 * QK_WIDTH, 2 * QK_WIDTH + ATT_WIDTH, 2 * QK_WIDTH + ATT_WIDTH + 2 * CONV_WIDTH)

kernel_name = "hybrid_diffattn_conformer_s5_block"


def rms_norm(x, g):
    x32 = x.astype(jnp.float32)
    y = x32 * lax.rsqrt(jnp.mean(jnp.square(x32), axis=-1, keepdims=True) + EPS)
    return (y * g.astype(jnp.float32)).astype(x.dtype)


def layer_norm(x, g, b):
    x32 = x.astype(jnp.float32)
    mu = jnp.mean(x32, axis=-1, keepdims=True)
    var = jnp.mean(jnp.square(x32 - mu), axis=-1, keepdims=True)
    y = (x32 - mu) * lax.rsqrt(var + EPS)
    return (y * g.astype(jnp.float32) + b.astype(jnp.float32)).astype(x.dtype)


def diff_attention(q, k, v, q_gain, k_gain, lam, out_gain, lam_init):
    bsz, seq = q.shape[0], q.shape[1]
    q = rms_norm(q, q_gain)
    k = rms_norm(k, k_gain)
    n_blk = seq // Q_BLOCK
    qb = q.reshape(bsz, n_blk, Q_BLOCK, N_ATT_HEADS, 2, ATT_HEAD_DIM).swapaxes(0, 1)
    kpos = jnp.arange(seq)
    kchunk = kpos // CHUNK
    slopes = 2.0 ** (-8.0 * jnp.arange(1, N_ATT_HEADS + 1, dtype=jnp.float32) / N_ATT_HEADS)
    scale = ATT_HEAD_DIM ** -0.5

    def block(args):
        qi, bi = args
        qpos = bi * Q_BLOCK + jnp.arange(Q_BLOCK)
        s = jnp.einsum('bqhcd,bkhcd->bhcqk', qi, k).astype(jnp.float32) * scale
        dist = jnp.abs(qpos[:, None] - kpos[None, :]).astype(jnp.float32)
        bias = -slopes[:, None, None, None] * dist[None, None]
        allowed = kchunk[None, :] <= (qpos // CHUNK)[:, None]
        s = jnp.where(allowed, s + bias, -1e30)
        pr = jax.nn.softmax(s, axis=-1)
        a = pr[:, :, 0] - lam * pr[:, :, 1]
        return jnp.einsum('bhqk,bkhd->bqhd', a.astype(v.dtype), v)

    o = lax.map(block, (qb, jnp.arange(n_blk)))
    o = o.swapaxes(0, 1).reshape(bsz, seq, N_ATT_HEADS, ATT_V_DIM)
    o = rms_norm(o, out_gain) * (1.0 - lam_init)
    return o.reshape(bsz, seq, N_ATT_HEADS * ATT_V_DIM)


def conformer_conv(u, w, b, ln_g, ln_b):
    a, gate = jnp.split(u, 2, axis=-1)
    hcv = a * jax.nn.sigmoid(gate)
    hcv = lax.conv_general_dilated(
        hcv, w[:, None, :], window_strides=(1,), padding=[(CONV_KERNEL - 1, 0)],
        dimension_numbers=('NWC', 'WIO', 'NWC'), feature_group_count=CONV_WIDTH) + b
    return jax.nn.silu(layer_norm(hcv, ln_g, ln_b))


def _ssm_combine(left, right):
    a_l, b_l = left
    a_r, b_r = right
    return a_l * a_r, a_r * b_l + b_r


def s5_ssm(u, a_re, a_im, log_dt, b_re, b_im, c_re, c_im, d_skip, glu_w, glu_b):
    bsz, seq = u.shape[0], u.shape[1]
    f32 = jnp.float32
    ug = u.reshape(bsz, seq, N_SSM_GROUPS, SSM_GROUP).astype(f32)
    lam = lax.complex(a_re.astype(f32), a_im.astype(f32))
    dt = jnp.exp(log_dt.astype(f32))[:, None]
    abar = jnp.exp(lam * dt)
    bmat = lax.complex(b_re.astype(f32), b_im.astype(f32))
    bbar = ((abar - 1.0) / lam)[..., None] * bmat
    bu = lax.complex(jnp.einsum('gpc,bsgc->bsgp', jnp.real(bbar), ug),
                     jnp.einsum('gpc,bsgc->bsgp', jnp.imag(bbar), ug))
    a_seq = jnp.broadcast_to(abar[None, None], (1, seq, N_SSM_GROUPS, SSM_STATE))
    _, xs = lax.associative_scan(_ssm_combine, (a_seq, bu), axis=1)
    y = (jnp.einsum('gcp,bsgp->bsgc', c_re.astype(f32), jnp.real(xs))
         - jnp.einsum('gcp,bsgp->bsgc', c_im.astype(f32), jnp.imag(xs))
         + d_skip.astype(f32) * ug)
    y = jax.nn.gelu(y.reshape(bsz, seq, SSM_WIDTH)).astype(u.dtype)
    return y * jax.nn.sigmoid(y @ glu_w + glu_b)


def setup_inputs(seed: int = 0) -> dict:
    key = jax.random.key(seed)
    ks = iter(jax.random.split(key, 40))
    L = DEPTH
    f32 = jnp.float32

    def nrm(shape, scale):
        return jax.random.normal(next(ks), shape, f32) * scale

    def gain(shape):
        return 1.0 + 0.05 * jax.random.normal(next(ks), shape, f32)

    G, P, C = N_SSM_GROUPS, SSM_STATE, SSM_GROUP
    return {
        "x": nrm((BATCH, SEQ, D_MODEL), 1.0),
        "p": nrm((DEPTH, BATCH, SEQ, PLE_DIM), 1.0),
        "mix_norm": gain((L, D_MODEL)),
        "w_in": nrm((L, D_MODEL, IN_WIDTH), D_MODEL ** -0.5),
        "q_norm": gain((L, ATT_HEAD_DIM)),
        "k_norm": gain((L, ATT_HEAD_DIM)),
        "lam_q1": nrm((L, ATT_HEAD_DIM), 0.1),
        "lam_k1": nrm((L, ATT_HEAD_DIM), 0.1),
        "lam_q2": nrm((L, ATT_HEAD_DIM), 0.1),
        "lam_k2": nrm((L, ATT_HEAD_DIM), 0.1),
        "att_out_norm": gain((L, ATT_V_DIM)),
        "conv_w": nrm((L, CONV_KERNEL, CONV_WIDTH), CONV_KERNEL ** -0.5),
        "conv_b": nrm((L, CONV_WIDTH), 0.02),
        "conv_ln_g": gain((L, CONV_WIDTH)),
        "conv_ln_b": nrm((L, CONV_WIDTH), 0.02),
        "ssm_a_re": -0.5 + nrm((L, G, P), 0.01),
        "ssm_a_im": math.pi * jnp.arange(P, dtype=f32)[None, None, :] + nrm((L, G, P), 0.01),
        "ssm_log_dt": jax.random.uniform(next(ks), (L, G), f32, math.log(1e-3), math.log(1e-1)),
        "ssm_b_re": nrm((L, G, P, C), (2 * C) ** -0.5),
        "ssm_b_im": nrm((L, G, P, C), (2 * C) ** -0.5),
        "ssm_c_re": nrm((L, G, C, P), P ** -0.5),
        "ssm_c_im": nrm((L, G, C, P), P ** -0.5),
        "ssm_d": nrm((L, G, C), 1.0),
        "ssm_glu_w": nrm((L, SSM_WIDTH, SSM_WIDTH), SSM_WIDTH ** -0.5),
        "ssm_glu_b": nrm((L, SSM_WIDTH), 0.02),
        "w_out": nrm((L, MIX_WIDTH, D_MODEL), MIX_WIDTH ** -0.5),
        "mlp_norm": gain((L, D_MODEL)),
        "w_ff1": nrm((L, D_MODEL, D_FF), D_MODEL ** -0.5),
        "w_ff2": nrm((L, D_FF, D_MODEL), D_FF ** -0.5),
        "ple_norm": gain((L, D_MODEL)),
        "w_ple_gate": nrm((L, D_MODEL, D_MODEL), D_MODEL ** -0.5),
        "w_ple_proj": nrm((L, PLE_DIM, D_MODEL), PLE_DIM ** -0.5),
    }


def reference(x, p, mix_norm, w_in, q_norm, k_norm, lam_q1, lam_k1, lam_q2, lam_k2,
              att_out_norm, conv_w, conv_b, conv_ln_g, conv_ln_b, ssm_a_re, ssm_a_im,
              ssm_log_dt, ssm_b_re, ssm_b_im, ssm_c_re, ssm_c_im, ssm_d, ssm_glu_w,
              ssm_glu_b, w_out, mlp_norm, w_ff1, w_ff2, ple_norm, w_ple_gate, w_ple_proj):
    bsz, seq, _ = x.shape
    h = x
    for i in range(DEPTH):
        lam_init = 0.8 - 0.6 * math.exp(-0.3 * i)
        n = rms_norm(h, mix_norm[i])
        z = n @ w_in[i]
        zq, zk, zv, zc, zs = jnp.split(z, SPLITS, axis=-1)
        q = zq.reshape(bsz, seq, N_ATT_HEADS, 2, ATT_HEAD_DIM)
        k = zk.reshape(bsz, seq, N_ATT_HEADS, 2, ATT_HEAD_DIM)
        v = zv.reshape(bsz, seq, N_ATT_HEADS, ATT_V_DIM)
        lam = (jnp.exp(jnp.sum(lam_q1[i].astype(jnp.float32) * lam_k1[i].astype(jnp.float32)))
               - jnp.exp(jnp.sum(lam_q2[i].astype(jnp.float32) * lam_k2[i].astype(jnp.float32)))
               + lam_init)
        att = diff_attention(q, k, v, q_norm[i], k_norm[i], lam, att_out_norm[i], lam_init)
        cnv = conformer_conv(zc, conv_w[i], conv_b[i], conv_ln_g[i], conv_ln_b[i])
        ssm = s5_ssm(zs, ssm_a_re[i], ssm_a_im[i], ssm_log_dt[i], ssm_b_re[i], ssm_b_im[i],
                     ssm_c_re[i], ssm_c_im[i], ssm_d[i], ssm_glu_w[i], ssm_glu_b[i])
        mixed = jnp.concatenate([att, cnv, ssm], axis=-1)
        h = h + mixed @ w_out[i]
        n2 = rms_norm(h, mlp_norm[i])
        h = h + jnp.square(jax.nn.relu(n2 @ w_ff1[i])) @ w_ff2[i]
        gate = jax.nn.sigmoid(rms_norm(h, ple_norm[i]) @ w_ple_gate[i])
        h = h + gate * (p[i] @ w_ple_proj[i])
    return h
```

```python
import functools
import math

import jax
import jax.numpy as jnp
from jax import lax
from jax.experimental import pallas as pl
from jax.experimental.pallas import tpu as pltpu

F32 = jnp.float32
BF16 = jnp.bfloat16

EPS = 1e-6
CHUNK = 64
HEAD_DIM = 64
V_DIM = 2 * HEAD_DIM
N_HEADS = 4
QK_WIDTH = N_HEADS * 2 * HEAD_DIM
ATT_WIDTH = N_HEADS * V_DIM
CONV_WIDTH = 256
CONV_KERNEL = 31
CONV_HALO = 32
SSM_WIDTH = 256
SSM_GROUP = 16
SSM_GROUPS = SSM_WIDTH // SSM_GROUP
SSM_STATE = 64
STATE_WIDTH = SSM_GROUPS * SSM_STATE

ROW_TILE = 512
ATT_TILE = 512
CONV_ROWS = 64
SSM_STEPS = 64
FF_CHUNK = 1024
VMEM_LIMIT = 56 * 1024 * 1024


def _rms(x, g):
    return x * lax.rsqrt(jnp.mean(x * x, axis=-1, keepdims=True) + EPS) * g


def _dot(a, b):
    return jnp.dot(a, b, preferred_element_type=F32)


def _sigmoid(x):
    return 1.0 / (1.0 + jnp.exp(-x))


def _pre_kernel(h_ref, g_ref, w_ref, qg_ref, kg_ref, ones_ref,
                q_ref, k_ref, v_ref, c_ref, s_ref):
    n = _rms(h_ref[...], g_ref[...])
    z = _dot(n.astype(BF16), w_ref[...])

    def head_norm(zz, gain):
        sq = zz * zz
        hi = sq.astype(BF16)
        lo = (sq - hi.astype(F32)).astype(BF16)
        ss = _dot(hi, ones_ref[...]) + _dot(lo, ones_ref[...])
        return zz * lax.rsqrt(ss * (1.0 / HEAD_DIM) + EPS) * gain

    o = 0
    q_ref[...] = head_norm(z[:, o:o + QK_WIDTH], qg_ref[...]).astype(BF16)
    o += QK_WIDTH
    k_ref[...] = head_norm(z[:, o:o + QK_WIDTH], kg_ref[...]).astype(BF16)
    o += QK_WIDTH
    v_ref[...] = z[:, o:o + ATT_WIDTH].astype(BF16)
    o += ATT_WIDTH
    a = z[:, o:o + CONV_WIDTH]
    gate = z[:, o + CONV_WIDTH:o + 2 * CONV_WIDTH]
    c_ref[...] = a * _sigmoid(gate)
    o += 2 * CONV_WIDTH
    s_ref[...] = z[:, o:o + SSM_WIDTH]


def _pre(h, g, w, qg, kg, ones):
    bsz, seq, d = h.shape
    tm = min(ROW_TILE, seq)
    grid = (bsz, seq // tm)
    full = lambda shape: pl.BlockSpec(shape, lambda b, s: (0,) * len(shape))
    tok = lambda width: pl.BlockSpec((None, tm, width), lambda b, s: (b, s, 0))
    return pl.pallas_call(
        _pre_kernel,
        grid=grid,
        in_specs=[tok(d), full(g.shape), full(w.shape), full(qg.shape), full(kg.shape), full(ones.shape)],
        out_specs=[tok(QK_WIDTH), tok(QK_WIDTH), tok(ATT_WIDTH), tok(CONV_WIDTH),
                   pl.BlockSpec((tm, SSM_WIDTH), lambda b, s: (s, b))],
        out_shape=[jax.ShapeDtypeStruct((bsz, seq, QK_WIDTH), BF16),
                   jax.ShapeDtypeStruct((bsz, seq, QK_WIDTH), BF16),
                   jax.ShapeDtypeStruct((bsz, seq, ATT_WIDTH), BF16),
                   jax.ShapeDtypeStruct((bsz, seq, CONV_WIDTH), F32),
                   jax.ShapeDtypeStruct((seq, bsz * SSM_WIDTH), F32)],
        compiler_params=pltpu.CompilerParams(
            dimension_semantics=("arbitrary", "arbitrary"), vmem_limit_bytes=VMEM_LIMIT),
        name="pre",
    )(h, g, w, qg, kg, ones)


def _attn_kernel(slopes_ref, scal_ref, lq1_ref, lk1_ref, lq2_ref, lk2_ref, og_ref,
                 q_ref, k_ref, v_ref, o_ref, acc_ref):
    tq = q_ref.shape[0]
    head = pl.program_id(1)
    i = pl.program_id(2)
    slope = slopes_ref[head]

    row = lax.broadcasted_iota(jnp.int32, (tq, tq), 0)
    col = lax.broadcasted_iota(jnp.int32, (tq, tq), 1)
    rel = (row - col).astype(F32)

    q = q_ref[...]
    lane = lax.broadcasted_iota(jnp.int32, q.shape, 1)
    zero = jnp.zeros_like(q)
    qs = (jnp.where(lane < HEAD_DIM, q, zero), jnp.where(lane >= HEAD_DIM, q, zero))

    def scores(qc, kt):
        return lax.dot_general(qc, kt, (((1,), (1,)), ((), ())), preferred_element_type=F32)

    start = pl.multiple_of(i * tq, tq)
    kd = k_ref[pl.ds(start, tq), :]
    vd = v_ref[pl.ds(start, tq), :]
    allowed = (col // CHUNK) <= (row // CHUNK)
    bias_d = -slope * jnp.abs(rel)
    init = []
    for c in range(2):
        s = jnp.where(allowed, scores(qs[c], kd) + bias_d, -1e30)
        m = jnp.max(s, axis=-1, keepdims=True)
        p = jnp.exp(s - m)
        init += [m, jnp.sum(p, axis=-1, keepdims=True)]
        acc_ref[c] = _dot(p.astype(BF16), vd)

    bias_o = -slope * rel

    def body(j, carry):
        kstart = pl.multiple_of(j * tq, tq)
        kt = k_ref[pl.ds(kstart, tq), :]
        vt = v_ref[pl.ds(kstart, tq), :]
        off = -slope * ((i - j) * tq).astype(F32)
        out = []
        for c in range(2):
            m_old, l_old = carry[2 * c], carry[2 * c + 1]
            s = scores(qs[c], kt) + bias_o + off
            m_new = jnp.maximum(m_old, jnp.max(s, axis=-1, keepdims=True))
            alpha = jnp.exp(m_old - m_new)
            p = jnp.exp(s - m_new)
            out += [m_new, alpha * l_old + jnp.sum(p, axis=-1, keepdims=True)]
            acc_ref[c] = alpha * acc_ref[c] + _dot(p.astype(BF16), vt)
        return tuple(out)

    m0, l0, m1, l1 = lax.fori_loop(0, i, body, tuple(init))

    lam_init = scal_ref[0]
    lam = (jnp.exp(jnp.sum(lq1_ref[...] * lk1_ref[...], axis=-1, keepdims=True))
           - jnp.exp(jnp.sum(lq2_ref[...] * lk2_ref[...], axis=-1, keepdims=True)) + lam_init)
    o = acc_ref[0] * (1.0 / l0) - lam * (acc_ref[1] * (1.0 / l1))
    o = _rms(o, og_ref[...]) * scal_ref[1]
    o_ref[...] = o.astype(o_ref.dtype)


def _attn(q, k, v, slopes, scal, lq1, lk1, lq2, lk2, og):
    bsz, seq, _ = q.shape
    tq = min(ATT_TILE, seq)
    grid = (bsz, N_HEADS, seq // tq)
    smem = pl.BlockSpec(memory_space=pltpu.SMEM)
    vec = lambda a: pl.BlockSpec(a.shape, lambda b, h, i: (0, 0))
    return pl.pallas_call(
        _attn_kernel,
        grid=grid,
        in_specs=[smem, smem, vec(lq1), vec(lk1), vec(lq2), vec(lk2), vec(og),
                  pl.BlockSpec((None, tq, V_DIM), lambda b, h, i: (b, i, h)),
                  pl.BlockSpec((None, seq, V_DIM), lambda b, h, i: (b, 0, h)),
                  pl.BlockSpec((None, seq, V_DIM), lambda b, h, i: (b, 0, h))],
        out_specs=pl.BlockSpec((None, tq, V_DIM), lambda b, h, i: (b, i, h)),
        out_shape=jax.ShapeDtypeStruct((bsz, seq, ATT_WIDTH), BF16),
        scratch_shapes=[pltpu.VMEM((2, tq, V_DIM), F32)],
        compiler_params=pltpu.CompilerParams(
            dimension_semantics=("arbitrary", "arbitrary", "arbitrary"), vmem_limit_bytes=VMEM_LIMIT),
        name="attn",
    )(slopes, scal, lq1, lk1, lq2, lk2, og, q, k, v)


def _conv_kernel(cur_ref, halo_ref, w_ref, b_ref, g_ref, beta_ref, o_ref, buf_ref):
    ts = cur_ref.shape[0]
    first = pl.program_id(1) == 0
    buf_ref[0:CONV_HALO, :] = jnp.where(first, 0.0, halo_ref[...])
    buf_ref[CONV_HALO:, :] = cur_ref[...]
    lead = CONV_HALO - (CONV_KERNEL - 1)
    for r in range(0, ts, CONV_ROWS):
        acc = jnp.zeros((CONV_ROWS, CONV_WIDTH), F32)
        for j in range(CONV_KERNEL):
            acc = acc + w_ref[j:j + 1, :] * buf_ref[r + lead + j:r + lead + j + CONV_ROWS, :]
        y = acc + b_ref[...]
        mu = jnp.mean(y, axis=-1, keepdims=True)
        d = y - mu
        var = jnp.mean(d * d, axis=-1, keepdims=True)
        yn = d * lax.rsqrt(var + EPS) * g_ref[...] + beta_ref[...]
        o_ref[r:r + CONV_ROWS, :] = (yn * _sigmoid(yn)).astype(o_ref.dtype)


def _conv(hc, w, b, g, beta):
    bsz, seq, _ = hc.shape
    ts = min(ROW_TILE, seq)
    per = ts // CONV_HALO
    grid = (bsz, seq // ts)
    vec = lambda a: pl.BlockSpec(a.shape, lambda b_, s: (0, 0))
    return pl.pallas_call(
        _conv_kernel,
        grid=grid,
        in_specs=[pl.BlockSpec((None, ts, CONV_WIDTH), lambda b_, s: (b_, s, 0)),
                  pl.BlockSpec((None, CONV_HALO, CONV_WIDTH),
                               lambda b_, s: (b_, jnp.maximum(s * per - 1, 0), 0)),
                  vec(w), vec(b), vec(g), vec(beta)],
        out_specs=pl.BlockSpec((None, ts, CONV_WIDTH), lambda b_, s: (b_, s, 0)),
        out_shape=jax.ShapeDtypeStruct((bsz, seq, CONV_WIDTH), BF16),
        scratch_shapes=[pltpu.VMEM((ts + CONV_HALO, CONV_WIDTH), F32)],
        compiler_params=pltpu.CompilerParams(
            dimension_semantics=("arbitrary", "arbitrary"), vmem_limit_bytes=VMEM_LIMIT),
        name="conv",
    )(hc, hc, w, b, g, beta)


def _ssm_kernel(u_ref, bmat_ref, cmat_ref, a_ref, d_ref, gw_ref, gb_ref, o_ref, x_ref, st_ref):
    nb = a_ref.shape[0]
    steps = u_ref.shape[0] // nb

    @pl.when(pl.program_id(0) == 0)
    def _():
        st_ref[...] = jnp.zeros_like(st_ref)

    u = u_ref[...]
    x_ref[...] = _dot(u.astype(BF16), bmat_ref[...])
    ar = a_ref[:, 0:STATE_WIDTH]
    ai = a_ref[:, STATE_WIDTH:]

    def step(t, carry):
        xr, xi = carry
        r = pl.multiple_of(t * nb, nb)
        nr = ar * xr - ai * xi + x_ref[pl.ds(r, nb), 0:STATE_WIDTH]
        ni = ar * xi + ai * xr + x_ref[pl.ds(r, nb), STATE_WIDTH:]
        x_ref[pl.ds(r, nb), 0:STATE_WIDTH] = nr
        x_ref[pl.ds(r, nb), STATE_WIDTH:] = ni
        return nr, ni

    xr, xi = lax.fori_loop(0, steps, step, (st_ref[:, 0:STATE_WIDTH], st_ref[:, STATE_WIDTH:]))
    st_ref[:, 0:STATE_WIDTH] = xr
    st_ref[:, STATE_WIDTH:] = xi

    y = _dot(x_ref[...].astype(BF16), cmat_ref[...]) + d_ref[...] * u
    y = 0.5 * y * (1.0 + jnp.tanh(math.sqrt(2.0 / math.pi) * (y + 0.044715 * (y * y * y))))
    gate = _dot(y.astype(BF16), gw_ref[...]) + gb_ref[...]
    o_ref[...] = (y * _sigmoid(gate)).astype(o_ref.dtype)


def _ssm(u, nb, bmat, cmat, a, d, gw, gb):
    rows = u.shape[0]
    tr = min(SSM_STEPS * nb, rows)
    full = lambda arr: pl.BlockSpec(arr.shape, lambda t: (0, 0))
    return pl.pallas_call(
        _ssm_kernel,
        grid=(rows // tr,),
        in_specs=[pl.BlockSpec((tr, SSM_WIDTH), lambda t: (t, 0)),
                  full(bmat), full(cmat), full(a), full(d), full(gw), full(gb)],
        out_specs=pl.BlockSpec((tr, SSM_WIDTH), lambda t: (t, 0)),
        out_shape=jax.ShapeDtypeStruct((rows, SSM_WIDTH), BF16),
        scratch_shapes=[pltpu.VMEM((tr, 2 * STATE_WIDTH), F32),
                        pltpu.VMEM((nb, 2 * STATE_WIDTH), F32)],
        compiler_params=pltpu.CompilerParams(
            dimension_semantics=("arbitrary",), vmem_limit_bytes=VMEM_LIMIT),
        name="ssm",
    )(u, bmat, cmat, a, d, gw, gb)


def _post_kernel(h_ref, att_ref, cnv_ref, ssm_ref, p_ref, wo_ref, g2_ref, w1_ref, w2_ref,
                 g3_ref, wg_ref, wp_ref, o_ref):
    mixed = jnp.concatenate([att_ref[...], cnv_ref[...], ssm_ref[...]], axis=-1)
    h = h_ref[...] + _dot(mixed, wo_ref[...])

    n2 = _rms(h, g2_ref[...]).astype(BF16)
    d_ff = w1_ref.shape[1]
    for c in range(0, d_ff, FF_CHUNK):
        t = jnp.maximum(_dot(n2, w1_ref[:, c:c + FF_CHUNK]), 0.0)
        h = h + _dot((t * t).astype(BF16), w2_ref[c:c + FF_CHUNK, :])

    gate = _sigmoid(_dot(_rms(h, g3_ref[...]).astype(BF16), wg_ref[...]))
    h = h + gate * _dot(p_ref[...], wp_ref[...])
    o_ref[...] = h


def _post(h, att, cnv, ssm, p, wo, g2, w1, w2, g3, wg, wp):
    bsz, seq, d = h.shape
    tm = min(ROW_TILE, seq)
    grid = (bsz, seq // tm)
    full = lambda a: pl.BlockSpec(a.shape, lambda b, s: (0, 0), pipeline_mode=pl.Buffered(1))
    tok = lambda width: pl.BlockSpec((None, tm, width), lambda b, s: (b, s, 0))
    return pl.pallas_call(
        _post_kernel,
        grid=grid,
        in_specs=[tok(d), tok(ATT_WIDTH), tok(CONV_WIDTH),
                  pl.BlockSpec((tm, SSM_WIDTH), lambda b, s: (s, b)),
                  tok(p.shape[-1]),
                  full(wo), full(g2), full(w1), full(w2), full(g3), full(wg), full(wp)],
        out_specs=tok(d),
        out_shape=jax.ShapeDtypeStruct(h.shape, F32),
        compiler_params=pltpu.CompilerParams(
            dimension_semantics=("arbitrary", "arbitrary"), vmem_limit_bytes=VMEM_LIMIT),
        name="post",
    )(h, att, cnv, ssm, p, wo, g2, w1, w2, g3, wg, wp)


def _ssm_params(a_re, a_im, log_dt, b_re, b_im, c_re, c_im, nb):
    lam = lax.complex(a_re, a_im)
    dt = jnp.exp(log_dt)[..., None]
    abar = jnp.exp(lam * dt)
    bbar = ((abar - 1.0) / lam)[..., None] * lax.complex(b_re, b_im)
    eye = jnp.eye(SSM_GROUPS, dtype=F32)
    nl = a_re.shape[0]

    def in_blockdiag(m):
        m = jnp.swapaxes(m, 2, 3)[:, :, :, None, :] * eye[None, :, None, :, None]
        return m.reshape(nl, SSM_WIDTH, STATE_WIDTH)

    def out_blockdiag(m):
        m = jnp.swapaxes(m, 2, 3)[:, :, :, None, :] * eye[None, :, None, :, None]
        return m.reshape(nl, STATE_WIDTH, SSM_WIDTH)

    bmat = jnp.concatenate([in_blockdiag(jnp.real(bbar)), in_blockdiag(jnp.imag(bbar))], axis=2)
    cmat = jnp.concatenate([out_blockdiag(c_re), -out_blockdiag(c_im)], axis=1)
    a = jnp.concatenate([jnp.real(abar).reshape(nl, 1, STATE_WIDTH),
                         jnp.imag(abar).reshape(nl, 1, STATE_WIDTH)], axis=2)
    a = jnp.broadcast_to(a, (nl, nb, 2 * STATE_WIDTH))
    return bmat.astype(BF16), cmat.astype(BF16), a


def kernel(x, p, mix_norm, w_in, q_norm, k_norm, lam_q1, lam_k1, lam_q2, lam_k2, att_out_norm, conv_w, conv_b, conv_ln_g, conv_ln_b, ssm_a_re, ssm_a_im, ssm_log_dt, ssm_b_re, ssm_b_im, ssm_c_re, ssm_c_im, ssm_d, ssm_glu_w, ssm_glu_b, w_out, mlp_norm, w_ff1, w_ff2, ple_norm, w_ple_gate, w_ple_proj):
    bsz, seq, d_model = x.shape
    depth = w_in.shape[0]
    assert bsz % 8 == 0 and seq % CHUNK == 0

    bmat, cmat, abar = _ssm_params(ssm_a_re, ssm_a_im, ssm_log_dt, ssm_b_re, ssm_b_im,
                                   ssm_c_re, ssm_c_im, bsz)
    group = jnp.arange(QK_WIDTH) // HEAD_DIM
    ones = (group[:, None] == group[None, :]).astype(BF16)
    slopes = 2.0 ** (-8.0 * jnp.arange(1, N_HEADS + 1, dtype=F32) / N_HEADS)
    scale = HEAD_DIM ** -0.5
    conv_w_pad = jnp.pad(conv_w, ((0, 0), (0, CONV_HALO - CONV_KERNEL), (0, 0)))
    p_bf = p.astype(BF16)
    row = lambda a: a.reshape(1, -1)

    h = x
    for i in range(depth):
        lam_init = 0.8 - 0.6 * math.exp(-0.3 * i)
        q, k, v, hc, zs = _pre(
            h, row(mix_norm[i]), w_in[i].astype(BF16),
            row(jnp.tile(q_norm[i], 2 * N_HEADS) * scale), row(jnp.tile(k_norm[i], 2 * N_HEADS)), ones)
        att = _attn(q, k, v, slopes, jnp.array([lam_init, 1.0 - lam_init], F32),
                    row(lam_q1[i]), row(lam_k1[i]), row(lam_q2[i]), row(lam_k2[i]), row(att_out_norm[i]))
        cnv = _conv(hc, conv_w_pad[i], row(conv_b[i]), row(conv_ln_g[i]), row(conv_ln_b[i]))
        ssm = _ssm(zs.reshape(seq * bsz, SSM_WIDTH), bsz, bmat[i], cmat[i], abar[i],
                   row(ssm_d[i]), ssm_glu_w[i].astype(BF16), row(ssm_glu_b[i]))
        h = _post(h, att, cnv, ssm.reshape(seq, bsz * SSM_WIDTH), p_bf[i],
                  w_out[i].astype(BF16), row(mlp_norm[i]), w_ff1[i].astype(BF16), w_ff2[i].astype(BF16),
                  row(ple_norm[i]), w_ple_gate[i].astype(BF16), w_ple_proj[i].astype(BF16))
    return h
```

```python
import functools
import math

import numpy as np
import jax
import jax.numpy as jnp
from jax import lax
from jax.experimental import pallas as pl
from jax.experimental.pallas import tpu as pltpu

F32 = jnp.float32
BF16 = jnp.bfloat16

EPS = 1e-6
CHUNK = 64
HEAD_DIM = 64
V_DIM = 2 * HEAD_DIM
N_HEADS = 4
QK_WIDTH = N_HEADS * 2 * HEAD_DIM
ATT_WIDTH = N_HEADS * V_DIM
CONV_WIDTH = 256
CONV_KERNEL = 31
CONV_HALO = 32
SSM_WIDTH = 256
SSM_GROUP = 16
SSM_GROUPS = SSM_WIDTH // SSM_GROUP
SSM_STATE = 64
STATE_WIDTH = SSM_GROUPS * SSM_STATE

ROW_TILE = 512
ATT_TILE = 512
CONV_ROWS = 64
SSM_STEPS = 64
FF_CHUNK = 1024
VMEM_LIMIT = 56 * 1024 * 1024


def _rms(x, g):
    return x * lax.rsqrt(jnp.mean(x * x, axis=-1, keepdims=True) + EPS) * g


def _dot(a, b):
    return jnp.dot(a, b, preferred_element_type=F32)


def _sigmoid(x):
    return 1.0 / (1.0 + jnp.exp(-x))


def _pre_kernel(h_ref, g_ref, w_ref, qg_ref, kg_ref, ones_ref,
                q_ref, k_ref, v_ref, c_ref, s_ref):
    n = _rms(h_ref[...], g_ref[...])
    z = _dot(n.astype(BF16), w_ref[...])

    def head_norm(zz, gain):
        sq = zz * zz
        hi = sq.astype(BF16)
        lo = (sq - hi.astype(F32)).astype(BF16)
        ss = _dot(hi, ones_ref[...]) + _dot(lo, ones_ref[...])
        return zz * lax.rsqrt(ss * (1.0 / HEAD_DIM) + EPS) * gain

    o = 0
    q_ref[...] = head_norm(z[:, o:o + QK_WIDTH], qg_ref[...]).astype(BF16)
    o += QK_WIDTH
    k_ref[...] = head_norm(z[:, o:o + QK_WIDTH], kg_ref[...]).astype(BF16)
    o += QK_WIDTH
    v_ref[...] = z[:, o:o + ATT_WIDTH].astype(BF16)
    o += ATT_WIDTH
    a = z[:, o:o + CONV_WIDTH]
    gate = z[:, o + CONV_WIDTH:o + 2 * CONV_WIDTH]
    c_ref[...] = a * _sigmoid(gate)
    o += 2 * CONV_WIDTH
    s_ref[...] = z[:, o:o + SSM_WIDTH]


def _pre(h, g, w, qg, kg, ones):
    bsz, seq, d = h.shape
    tm = min(ROW_TILE, seq)
    grid = (bsz, seq // tm)
    full = lambda shape: pl.BlockSpec(shape, lambda b, s: (0,) * len(shape))
    tok = lambda width: pl.BlockSpec((None, tm, width), lambda b, s: (b, s, 0))
    return pl.pallas_call(
        _pre_kernel,
        grid=grid,
        in_specs=[tok(d), full(g.shape), full(w.shape), full(qg.shape), full(kg.shape), full(ones.shape)],
        out_specs=[tok(QK_WIDTH), tok(QK_WIDTH), tok(ATT_WIDTH), tok(CONV_WIDTH),
                   pl.BlockSpec((tm, SSM_WIDTH), lambda b, s: (s, b))],
        out_shape=[jax.ShapeDtypeStruct((bsz, seq, QK_WIDTH), BF16),
                   jax.ShapeDtypeStruct((bsz, seq, QK_WIDTH), BF16),
                   jax.ShapeDtypeStruct((bsz, seq, ATT_WIDTH), BF16),
                   jax.ShapeDtypeStruct((bsz, seq, CONV_WIDTH), F32),
                   jax.ShapeDtypeStruct((seq, bsz * SSM_WIDTH), F32)],
        compiler_params=pltpu.CompilerParams(
            dimension_semantics=("arbitrary", "arbitrary"), vmem_limit_bytes=VMEM_LIMIT),
        name="pre",
    )(h, g, w, qg, kg, ones)


def _attn_kernel(slopes_ref, scal_ref, lq1_ref, lk1_ref, lq2_ref, lk2_ref, og_ref,
                 q_ref, k_ref, kb_ref, v_ref, o_ref, qq_ref, sd_ref, sa_ref, sb_ref, acc_ref):
    tq = q_ref.shape[0]
    head = pl.program_id(1)
    i = pl.program_id(2)
    slope = slopes_ref[head]

    q = q_ref[...]
    lane = lax.broadcasted_iota(jnp.int32, q.shape, 1)
    zero = jnp.zeros_like(q)
    pieces = jnp.zeros(q.shape, F32)
    for col_, piece in enumerate(LOG2E_PIECES + LOG2E_PIECES):
        pieces = jnp.where(lane == col_, piece, pieces)
    pieces = pieces.astype(q.dtype)
    qq_ref[0:tq, :] = jnp.concatenate([jnp.where(lane < HEAD_DIM, q, zero), pieces], axis=1)
    qq_ref[tq:, :] = jnp.concatenate([jnp.where(lane >= HEAD_DIM, q, zero), pieces], axis=1)
    acc_ref[...] = jnp.zeros_like(acc_ref)

    def scores(t, s_ref):
        start = pl.multiple_of(t * tq, tq)
        kt = jnp.concatenate([k_ref[pl.ds(start, tq), :], kb_ref[pl.ds(start, tq), :]], axis=1)
        s_ref[...] = lax.dot_general(qq_ref[...], kt, (((1,), (1,)), ((), ())), preferred_element_type=F32)

    def update(t, s_ref, carry, fix):
        vt = v_ref[pl.ds(pl.multiple_of(t * tq, tq), tq), :]
        out = []
        for c in range(2):
            m_old, l_old = carry[2 * c], carry[2 * c + 1]
            s = s_ref[c * tq:(c + 1) * tq, :]
            if fix is not None:
                s = s + fix
            m_new = jnp.maximum(m_old, jnp.max(s, axis=-1, keepdims=True))
            alpha = jnp.exp2(m_old - m_new)
            p = jnp.exp2(s - m_new)
            out += [m_new, alpha * l_old + jnp.sum(p, axis=-1, keepdims=True)]
            acc_ref[c] = alpha * acc_ref[c] + _dot(p.astype(BF16), vt)
        return tuple(out)

    scores(i, sd_ref)
    scores(0, sa_ref)
    row = lax.broadcasted_iota(jnp.int32, (tq, tq), 0)
    col = lax.broadcasted_iota(jnp.int32, (tq, tq), 1)
    fix = jnp.minimum((2.0 * slope) * (row - col).astype(F32), 0.0)
    fix = jnp.where((col // CHUNK) <= (row // CHUNK), fix, -1e30)
    neg = jnp.full((tq, 1), -1e30, F32)
    nil = jnp.zeros((tq, 1), F32)
    carry = update(i, sd_ref, (neg, nil, neg, nil), fix)

    def pair(u, carry):
        scores(2 * u + 1, sb_ref)
        carry = update(2 * u, sa_ref, carry, None)
        scores(2 * u + 2, sa_ref)
        return update(2 * u + 1, sb_ref, carry, None)

    carry = lax.fori_loop(0, i // 2, pair, carry)
    m0, l0, m1, l1 = lax.cond(i % 2 == 1, lambda cr: update(i - 1, sa_ref, cr, None), lambda cr: cr, carry)

    lam_init = scal_ref[0]
    lam = (jnp.exp(jnp.sum(lq1_ref[...] * lk1_ref[...], axis=-1, keepdims=True))
           - jnp.exp(jnp.sum(lq2_ref[...] * lk2_ref[...], axis=-1, keepdims=True)) + lam_init)
    o = acc_ref[0] * (1.0 / l0) - lam * (acc_ref[1] * (1.0 / l1))
    o = _rms(o, og_ref[...]) * scal_ref[1]
    o_ref[...] = o.astype(o_ref.dtype)


def _bf16_pieces(x, n):
    out = []
    for _ in range(n):
        piece = float(np.asarray(x, np.float32).astype(BF16).astype(np.float32))
        out.append(piece)
        x -= piece
    return tuple(out)


LOG2E_PIECES = _bf16_pieces(math.log2(math.e), 3)
BIAS_COLS = 2 * len(LOG2E_PIECES)


def _key_bias(slopes, seq):
    kpos = jnp.arange(seq, dtype=jnp.int32)
    lo = (kpos % CHUNK).astype(F32)
    hi = (kpos - kpos % CHUNK).astype(F32)
    n = len(LOG2E_PIECES)
    cols = jnp.stack([hi] * n + [lo] * n, axis=-1)[None] * slopes[:, None, None]
    return jnp.pad(cols.astype(BF16), ((0, 0), (0, 0), (0, V_DIM - BIAS_COLS)))


def _attn(q, k, kbias, v, slopes, scal, lq1, lk1, lq2, lk2, og):
    bsz, seq, _ = q.shape
    tq = min(ATT_TILE, seq)
    grid = (bsz, N_HEADS, seq // tq)
    smem = pl.BlockSpec(memory_space=pltpu.SMEM)
    vec = lambda a: pl.BlockSpec(a.shape, lambda b, h, i: (0, 0))
    return pl.pallas_call(
        _attn_kernel,
        grid=grid,
        in_specs=[smem, smem, vec(lq1), vec(lk1), vec(lq2), vec(lk2), vec(og),
                  pl.BlockSpec((None, tq, V_DIM), lambda b, h, i: (b, i, h)),
                  pl.BlockSpec((None, seq, V_DIM), lambda b, h, i: (b, 0, h)),
                  pl.BlockSpec((None, seq, V_DIM), lambda b, h, i: (h, 0, 0)),
                  pl.BlockSpec((None, seq, V_DIM), lambda b, h, i: (b, 0, h))],
        out_specs=pl.BlockSpec((None, tq, V_DIM), lambda b, h, i: (b, i, h)),
        out_shape=jax.ShapeDtypeStruct((bsz, seq, ATT_WIDTH), BF16),
        scratch_shapes=[pltpu.VMEM((2 * tq, 2 * V_DIM), BF16),
                        pltpu.VMEM((2 * tq, tq), F32),
                        pltpu.VMEM((2 * tq, tq), F32),
                        pltpu.VMEM((2 * tq, tq), F32),
                        pltpu.VMEM((2, tq, V_DIM), F32)],
        compiler_params=pltpu.CompilerParams(
            dimension_semantics=("arbitrary", "arbitrary", "arbitrary"), vmem_limit_bytes=VMEM_LIMIT),
        name="attn",
    )(slopes, scal, lq1, lk1, lq2, lk2, og, q, k, kbias, v)


def _conv_kernel(cur_ref, halo_ref, w_ref, b_ref, g_ref, beta_ref, o_ref, buf_ref):
    ts = cur_ref.shape[0]
    first = pl.program_id(1) == 0
    buf_ref[0:CONV_HALO, :] = jnp.where(first, 0.0, halo_ref[...])
    buf_ref[CONV_HALO:, :] = cur_ref[...]
    lead = CONV_HALO - (CONV_KERNEL - 1)
    for r in range(0, ts, CONV_ROWS):
        acc = jnp.zeros((CONV_ROWS, CONV_WIDTH), F32)
        for j in range(CONV_KERNEL):
            acc = acc + w_ref[j:j + 1, :] * buf_ref[r + lead + j:r + lead + j + CONV_ROWS, :]
        y = acc + b_ref[...]
        mu = jnp.mean(y, axis=-1, keepdims=True)
        d = y - mu
        var = jnp.mean(d * d, axis=-1, keepdims=True)
        yn = d * lax.rsqrt(var + EPS) * g_ref[...] + beta_ref[...]
        o_ref[r:r + CONV_ROWS, :] = (yn * _sigmoid(yn)).astype(o_ref.dtype)


def _conv(hc, w, b, g, beta):
    bsz, seq, _ = hc.shape
    ts = min(ROW_TILE, seq)
    per = ts // CONV_HALO
    grid = (bsz, seq // ts)
    vec = lambda a: pl.BlockSpec(a.shape, lambda b_, s: (0, 0))
    return pl.pallas_call(
        _conv_kernel,
        grid=grid,
        in_specs=[pl.BlockSpec((None, ts, CONV_WIDTH), lambda b_, s: (b_, s, 0)),
                  pl.BlockSpec((None, CONV_HALO, CONV_WIDTH),
                               lambda b_, s: (b_, jnp.maximum(s * per - 1, 0), 0)),
                  vec(w), vec(b), vec(g), vec(beta)],
        out_specs=pl.BlockSpec((None, ts, CONV_WIDTH), lambda b_, s: (b_, s, 0)),
        out_shape=jax.ShapeDtypeStruct((bsz, seq, CONV_WIDTH), BF16),
        scratch_shapes=[pltpu.VMEM((ts + CONV_HALO, CONV_WIDTH), F32)],
        compiler_params=pltpu.CompilerParams(
            dimension_semantics=("arbitrary", "arbitrary"), vmem_limit_bytes=VMEM_LIMIT),
        name="conv",
    )(hc, hc, w, b, g, beta)


def _ssm_kernel(u_ref, bmat_ref, cmat_ref, a_ref, d_ref, gw_ref, gb_ref, o_ref, x_ref, st_ref):
    nb = a_ref.shape[0]
    steps = u_ref.shape[0] // nb

    @pl.when(pl.program_id(0) == 0)
    def _():
        st_ref[...] = jnp.zeros_like(st_ref)

    u = u_ref[...]
    x_ref[...] = _dot(u.astype(BF16), bmat_ref[...])
    ar = a_ref[:, 0:STATE_WIDTH]
    ai = a_ref[:, STATE_WIDTH:]

    def step(t, carry):
        xr, xi = carry
        r = pl.multiple_of(t * nb, nb)
        nr = ar * xr - ai * xi + x_ref[pl.ds(r, nb), 0:STATE_WIDTH]
        ni = ar * xi + ai * xr + x_ref[pl.ds(r, nb), STATE_WIDTH:]
        x_ref[pl.ds(r, nb), 0:STATE_WIDTH] = nr
        x_ref[pl.ds(r, nb), STATE_WIDTH:] = ni
        return nr, ni

    xr, xi = lax.fori_loop(0, steps, step, (st_ref[:, 0:STATE_WIDTH], st_ref[:, STATE_WIDTH:]))
    st_ref[:, 0:STATE_WIDTH] = xr
    st_ref[:, STATE_WIDTH:] = xi

    y = _dot(x_ref[...].astype(BF16), cmat_ref[...]) + d_ref[...] * u
    y = 0.5 * y * (1.0 + jnp.tanh(math.sqrt(2.0 / math.pi) * (y + 0.044715 * (y * y * y))))
    gate = _dot(y.astype(BF16), gw_ref[...]) + gb_ref[...]
    o_ref[...] = (y * _sigmoid(gate)).astype(o_ref.dtype)


def _ssm(u, nb, bmat, cmat, a, d, gw, gb):
    rows = u.shape[0]
    tr = min(SSM_STEPS * nb, rows)
    full = lambda arr: pl.BlockSpec(arr.shape, lambda t: (0, 0))
    return pl.pallas_call(
        _ssm_kernel,
        grid=(rows // tr,),
        in_specs=[pl.BlockSpec((tr, SSM_WIDTH), lambda t: (t, 0)),
                  full(bmat), full(cmat), full(a), full(d), full(gw), full(gb)],
        out_specs=pl.BlockSpec((tr, SSM_WIDTH), lambda t: (t, 0)),
        out_shape=jax.ShapeDtypeStruct((rows, SSM_WIDTH), BF16),
        scratch_shapes=[pltpu.VMEM((tr, 2 * STATE_WIDTH), F32),
                        pltpu.VMEM((nb, 2 * STATE_WIDTH), F32)],
        compiler_params=pltpu.CompilerParams(
            dimension_semantics=("arbitrary",), vmem_limit_bytes=VMEM_LIMIT),
        name="ssm",
    )(u, bmat, cmat, a, d, gw, gb)


def _post_kernel(h_ref, att_ref, cnv_ref, ssm_ref, p_ref, wo_ref, g2_ref, w1_ref, w2_ref,
                 g3_ref, wg_ref, wp_ref, o_ref):
    mixed = jnp.concatenate([att_ref[...], cnv_ref[...], ssm_ref[...]], axis=-1)
    h = h_ref[...] + _dot(mixed, wo_ref[...])

    n2 = _rms(h, g2_ref[...]).astype(BF16)
    d_ff = w1_ref.shape[1]
    for c in range(0, d_ff, FF_CHUNK):
        t = jnp.maximum(_dot(n2, w1_ref[:, c:c + FF_CHUNK]), 0.0)
        h = h + _dot((t * t).astype(BF16), w2_ref[c:c + FF_CHUNK, :])

    gate = _sigmoid(_dot(_rms(h, g3_ref[...]).astype(BF16), wg_ref[...]))
    h = h + gate * _dot(p_ref[...], wp_ref[...])
    o_ref[...] = h


def _post(h, att, cnv, ssm, p, wo, g2, w1, w2, g3, wg, wp):
    bsz, seq, d = h.shape
    tm = min(ROW_TILE, seq)
    grid = (bsz, seq // tm)
    full = lambda a: pl.BlockSpec(a.shape, lambda b, s: (0, 0), pipeline_mode=pl.Buffered(1))
    tok = lambda width: pl.BlockSpec((None, tm, width), lambda b, s: (b, s, 0))
    return pl.pallas_call(
        _post_kernel,
        grid=grid,
        in_specs=[tok(d), tok(ATT_WIDTH), tok(CONV_WIDTH),
                  pl.BlockSpec((tm, SSM_WIDTH), lambda b, s: (s, b)),
                  tok(p.shape[-1]),
                  full(wo), full(g2), full(w1), full(w2), full(g3), full(wg), full(wp)],
        out_specs=tok(d),
        out_shape=jax.ShapeDtypeStruct(h.shape, F32),
        compiler_params=pltpu.CompilerParams(
            dimension_semantics=("arbitrary", "arbitrary"), vmem_limit_bytes=VMEM_LIMIT),
        name="post",
    )(h, att, cnv, ssm, p, wo, g2, w1, w2, g3, wg, wp)


def _ssm_params(a_re, a_im, log_dt, b_re, b_im, c_re, c_im, nb):
    lam = lax.complex(a_re, a_im)
    dt = jnp.exp(log_dt)[..., None]
    abar = jnp.exp(lam * dt)
    bbar = ((abar - 1.0) / lam)[..., None] * lax.complex(b_re, b_im)
    eye = jnp.eye(SSM_GROUPS, dtype=F32)
    nl = a_re.shape[0]

    def in_blockdiag(m):
        m = jnp.swapaxes(m, 2, 3)[:, :, :, None, :] * eye[None, :, None, :, None]
        return m.reshape(nl, SSM_WIDTH, STATE_WIDTH)

    def out_blockdiag(m):
        m = jnp.swapaxes(m, 2, 3)[:, :, :, None, :] * eye[None, :, None, :, None]
        return m.reshape(nl, STATE_WIDTH, SSM_WIDTH)

    bmat = jnp.concatenate([in_blockdiag(jnp.real(bbar)), in_blockdiag(jnp.imag(bbar))], axis=2)
    cmat = jnp.concatenate([out_blockdiag(c_re), -out_blockdiag(c_im)], axis=1)
    a = jnp.concatenate([jnp.real(abar).reshape(nl, 1, STATE_WIDTH),
                         jnp.imag(abar).reshape(nl, 1, STATE_WIDTH)], axis=2)
    a = jnp.broadcast_to(a, (nl, nb, 2 * STATE_WIDTH))
    return bmat.astype(BF16), cmat.astype(BF16), a


def kernel(x, p, mix_norm, w_in, q_norm, k_norm, lam_q1, lam_k1, lam_q2, lam_k2, att_out_norm, conv_w, conv_b, conv_ln_g, conv_ln_b, ssm_a_re, ssm_a_im, ssm_log_dt, ssm_b_re, ssm_b_im, ssm_c_re, ssm_c_im, ssm_d, ssm_glu_w, ssm_glu_b, w_out, mlp_norm, w_ff1, w_ff2, ple_norm, w_ple_gate, w_ple_proj):
    bsz, seq, d_model = x.shape
    depth = w_in.shape[0]
    assert bsz % 8 == 0 and seq % CHUNK == 0

    bmat, cmat, abar = _ssm_params(ssm_a_re, ssm_a_im, ssm_log_dt, ssm_b_re, ssm_b_im,
                                   ssm_c_re, ssm_c_im, bsz)
    group = jnp.arange(QK_WIDTH) // HEAD_DIM
    ones = (group[:, None] == group[None, :]).astype(BF16)
    log2e = math.log2(math.e)
    slopes = 2.0 ** (-8.0 * jnp.arange(1, N_HEADS + 1, dtype=F32) / N_HEADS)
    kbias = _key_bias(slopes, seq)
    slopes = slopes * log2e
    scale = HEAD_DIM ** -0.5 * log2e
    conv_w_pad = jnp.pad(conv_w, ((0, 0), (0, CONV_HALO - CONV_KERNEL), (0, 0)))
    p_bf = p.astype(BF16)
    row = lambda a: a.reshape(1, -1)

    h = x
    for i in range(depth):
        lam_init = 0.8 - 0.6 * math.exp(-0.3 * i)
        q, k, v, hc, zs = _pre(
            h, row(mix_norm[i]), w_in[i].astype(BF16),
            row(jnp.tile(q_norm[i], 2 * N_HEADS) * scale), row(jnp.tile(k_norm[i], 2 * N_HEADS)), ones)
        att = _attn(q, k, kbias, v, slopes, jnp.array([lam_init, 1.0 - lam_init], F32),
                    row(lam_q1[i]), row(lam_k1[i]), row(lam_q2[i]), row(lam_k2[i]), row(att_out_norm[i]))
        cnv = _conv(hc, conv_w_pad[i], row(conv_b[i]), row(conv_ln_g[i]), row(conv_ln_b[i]))
        ssm = _ssm(zs.reshape(seq * bsz, SSM_WIDTH), bsz, bmat[i], cmat[i], abar[i],
                   row(ssm_d[i]), ssm_glu_w[i].astype(BF16), row(ssm_glu_b[i]))
        h = _post(h, att, cnv, ssm.reshape(seq, bsz * SSM_WIDTH), p_bf[i],
                  w_out[i].astype(BF16), row(mlp_norm[i]), w_ff1[i].astype(BF16), w_ff2[i].astype(BF16),
                  row(ple_norm[i]), w_ple_gate[i].astype(BF16), w_ple_proj[i].astype(BF16))
    return h
```

```python
import functools
import math

import numpy as np
import jax
import jax.numpy as jnp
from jax import lax
from jax.experimental import pallas as pl
from jax.experimental.pallas import tpu as pltpu

F32 = jnp.float32
BF16 = jnp.bfloat16

EPS = 1e-6
CHUNK = 64
HEAD_DIM = 64
V_DIM = 2 * HEAD_DIM
N_HEADS = 4
QK_WIDTH = N_HEADS * 2 * HEAD_DIM
ATT_WIDTH = N_HEADS * V_DIM
CONV_WIDTH = 256
CONV_KERNEL = 31
CONV_HALO = 32
SSM_WIDTH = 256
SSM_GROUP = 16
SSM_GROUPS = SSM_WIDTH // SSM_GROUP
SSM_STATE = 64
STATE_WIDTH = SSM_GROUPS * SSM_STATE

ROW_TILE = 512
ATT_TILE = 512
CONV_ROWS = 64
SSM_STEPS = 64
FF_CHUNK = 1024
VMEM_LIMIT = 56 * 1024 * 1024


def _rms(x, g):
    return x * lax.rsqrt(jnp.mean(x * x, axis=-1, keepdims=True) + EPS) * g


def _dot(a, b):
    return jnp.dot(a, b, preferred_element_type=F32)


def _sigmoid(x):
    return 1.0 / (1.0 + jnp.exp(-x))


def _pre_kernel(h_ref, g_ref, w_ref, qg_ref, kg_ref, ones_ref, cw_ref, cb_ref, cg_ref, cbeta_ref,
                q_ref, k_ref, v_ref, c_ref, s_ref, buf_ref, sh_ref):
    tm = h_ref.shape[0]
    n = _rms(h_ref[...], g_ref[...]).astype(BF16)
    col = {"q": 0, "k": QK_WIDTH, "v": 2 * QK_WIDTH, "c": 2 * QK_WIDTH + ATT_WIDTH,
           "s": 2 * QK_WIDTH + ATT_WIDTH + 2 * CONV_WIDTH}

    def proj(name, width):
        return _dot(n, w_ref[:, col[name]:col[name] + width])

    first = pl.program_id(1) == 0

    @pl.when(first)
    def _():
        buf_ref[0:CONV_HALO, :] = jnp.zeros((CONV_HALO, CONV_WIDTH), F32)

    @pl.when(jnp.logical_not(first))
    def _():
        buf_ref[0:CONV_HALO, :] = buf_ref[tm:tm + CONV_HALO, :]

    zc = proj("c", 2 * CONV_WIDTH)
    buf_ref[CONV_HALO:, :] = zc[:, :CONV_WIDTH] * _sigmoid(zc[:, CONV_WIDTH:])
    span = sh_ref.shape[1]
    for r in range(1, 8):
        sh_ref[r - 1] = buf_ref[r:r + span, :]
    lead = CONV_HALO - (CONV_KERNEL - 1)
    groups = CONV_ROWS // 8

    def conv_rows(rb):
        acc = jnp.zeros((groups, 8, CONV_WIDTH), F32)
        for j in range(CONV_KERNEL):
            m8, r = divmod(lead + j, 8)
            if r == 0:
                src = buf_ref[rb + 8 * m8:rb + 8 * m8 + CONV_ROWS, :]
            else:
                src = sh_ref[r - 1, rb + 8 * m8:rb + 8 * m8 + CONV_ROWS, :]
            acc = acc + cw_ref[8 * j:8 * j + 8, :][None] * src.reshape(groups, 8, CONV_WIDTH)
        y = acc.reshape(CONV_ROWS, CONV_WIDTH) + cb_ref[...]
        mu = jnp.mean(y, axis=-1, keepdims=True)
        d = y - mu
        var = jnp.mean(d * d, axis=-1, keepdims=True)
        yn = d * lax.rsqrt(var + EPS) * cg_ref[...] + cbeta_ref[...]
        c_ref[rb:rb + CONV_ROWS, :] = (yn * _sigmoid(yn)).astype(c_ref.dtype)

    def head_norm(zz, gain):
        sq = (zz * zz).astype(BF16)
        return zz * lax.rsqrt(_dot(sq, ones_ref[...]) * (1.0 / HEAD_DIM) + EPS) * gain

    q_ref[...] = head_norm(proj("q", QK_WIDTH), qg_ref[...]).astype(BF16)
    k_ref[...] = head_norm(proj("k", QK_WIDTH), kg_ref[...]).astype(BF16)
    v_ref[...] = proj("v", ATT_WIDTH).astype(BF16)
    s_ref[...] = proj("s", SSM_WIDTH)

    for rb in range(0, tm, CONV_ROWS):
        conv_rows(rb)


def _pre(h, g, w, qg, kg, ones, cw, cb, cg, cbeta):
    bsz, seq, d = h.shape
    tm = min(ROW_TILE, seq)
    grid = (bsz, seq // tm)
    full = lambda a: pl.BlockSpec(a.shape, lambda b, s: (0,) * a.ndim)
    tok = lambda width: pl.BlockSpec((None, tm, width), lambda b, s: (b, s, 0))
    return pl.pallas_call(
        _pre_kernel,
        grid=grid,
        in_specs=[tok(d)] + [full(a) for a in (g, w, qg, kg, ones, cw, cb, cg, cbeta)],
        out_specs=[tok(QK_WIDTH), tok(QK_WIDTH), tok(ATT_WIDTH), tok(CONV_WIDTH), tok(SSM_WIDTH)],
        out_shape=[jax.ShapeDtypeStruct((bsz, seq, QK_WIDTH), BF16),
                   jax.ShapeDtypeStruct((bsz, seq, QK_WIDTH), BF16),
                   jax.ShapeDtypeStruct((bsz, seq, ATT_WIDTH), BF16),
                   jax.ShapeDtypeStruct((bsz, seq, CONV_WIDTH), BF16),
                   jax.ShapeDtypeStruct((bsz, seq, SSM_WIDTH), F32)],
        scratch_shapes=[pltpu.VMEM((tm + CONV_HALO, CONV_WIDTH), F32),
                        pltpu.VMEM((7, tm + CONV_HALO - 8, CONV_WIDTH), F32)],
        compiler_params=pltpu.CompilerParams(
            dimension_semantics=("arbitrary", "arbitrary"), vmem_limit_bytes=VMEM_LIMIT),
        name="pre",
    )(h, g, w, qg, kg, ones, cw, cb, cg, cbeta)


def _attn_kernel(slopes_ref, scal_ref, lq1_ref, lk1_ref, lq2_ref, lk2_ref, og_ref,
                 q_ref, k_ref, kb_ref, v_ref, o_ref, qq_ref, sd_ref, sa_ref, sb_ref, acc_ref):
    tq = q_ref.shape[0]
    head = pl.program_id(1)
    i = pl.program_id(2)
    slope = slopes_ref[head]

    q = q_ref[...]
    lane = lax.broadcasted_iota(jnp.int32, q.shape, 1)
    zero = jnp.zeros_like(q)
    pieces = jnp.zeros(q.shape, F32)
    for col_, piece in enumerate(LOG2E_PIECES + LOG2E_PIECES):
        pieces = jnp.where(lane == col_, piece, pieces)
    pieces = pieces.astype(q.dtype)
    qq_ref[0:tq, :] = jnp.concatenate([jnp.where(lane < HEAD_DIM, q, zero), pieces], axis=1)
    qq_ref[tq:, :] = jnp.concatenate([jnp.where(lane >= HEAD_DIM, q, zero), pieces], axis=1)
    acc_ref[...] = jnp.zeros_like(acc_ref)

    def scores(t, s_ref):
        start = pl.multiple_of(t * tq, tq)
        kt = jnp.concatenate([k_ref[pl.ds(start, tq), :], kb_ref[pl.ds(start, tq), :]], axis=1)
        s_ref[...] = lax.dot_general(qq_ref[...], kt, (((1,), (1,)), ((), ())), preferred_element_type=F32)

    def update(t, s_ref, carry, fix):
        vt = v_ref[pl.ds(pl.multiple_of(t * tq, tq), tq), :]
        out = []
        for c in range(2):
            m_old, l_old = carry[2 * c], carry[2 * c + 1]
            s = s_ref[c * tq:(c + 1) * tq, :]
            if fix is not None:
                s = s + fix
            m_new = jnp.maximum(m_old, jnp.max(s, axis=-1, keepdims=True))
            alpha = jnp.exp2(m_old - m_new)
            p = jnp.exp2(s - m_new)
            out += [m_new, alpha * l_old + jnp.sum(p, axis=-1, keepdims=True)]
            acc_ref[c] = alpha * acc_ref[c] + _dot(p.astype(BF16), vt)
        return tuple(out)

    scores(i, sd_ref)
    scores(0, sa_ref)
    row = lax.broadcasted_iota(jnp.int32, (tq, tq), 0)
    col = lax.broadcasted_iota(jnp.int32, (tq, tq), 1)
    fix = jnp.minimum((2.0 * slope) * (row - col).astype(F32), 0.0)
    fix = jnp.where((col // CHUNK) <= (row // CHUNK), fix, -1e30)
    neg = jnp.full((tq, 1), -1e30, F32)
    nil = jnp.zeros((tq, 1), F32)
    carry = update(i, sd_ref, (neg, nil, neg, nil), fix)

    def pair(u, carry):
        scores(2 * u + 1, sb_ref)
        carry = update(2 * u, sa_ref, carry, None)
        scores(2 * u + 2, sa_ref)
        return update(2 * u + 1, sb_ref, carry, None)

    carry = lax.fori_loop(0, i // 2, pair, carry)
    m0, l0, m1, l1 = lax.cond(i % 2 == 1, lambda cr: update(i - 1, sa_ref, cr, None), lambda cr: cr, carry)

    lam_init = scal_ref[0]
    lam = (jnp.exp(jnp.sum(lq1_ref[...] * lk1_ref[...], axis=-1, keepdims=True))
           - jnp.exp(jnp.sum(lq2_ref[...] * lk2_ref[...], axis=-1, keepdims=True)) + lam_init)
    o = acc_ref[0] * (1.0 / l0) - lam * (acc_ref[1] * (1.0 / l1))
    o = _rms(o, og_ref[...]) * scal_ref[1]
    o_ref[...] = o.astype(o_ref.dtype)


def _bf16_pieces(x, n):
    out = []
    for _ in range(n):
        piece = float(np.asarray(x, np.float32).astype(BF16).astype(np.float32))
        out.append(piece)
        x -= piece
    return tuple(out)


LOG2E_PIECES = _bf16_pieces(math.log2(math.e), 3)
BIAS_COLS = 2 * len(LOG2E_PIECES)


def _key_bias(slopes, seq):
    kpos = jnp.arange(seq, dtype=jnp.int32)
    lo = (kpos % CHUNK).astype(F32)
    hi = (kpos - kpos % CHUNK).astype(F32)
    n = len(LOG2E_PIECES)
    cols = jnp.stack([hi] * n + [lo] * n, axis=-1)[None] * slopes[:, None, None]
    return jnp.pad(cols.astype(BF16), ((0, 0), (0, 0), (0, V_DIM - BIAS_COLS)))


def _attn(q, k, kbias, v, slopes, scal, lq1, lk1, lq2, lk2, og):
    bsz, seq, _ = q.shape
    tq = min(ATT_TILE, seq)
    grid = (bsz, N_HEADS, seq // tq)
    smem = pl.BlockSpec(memory_space=pltpu.SMEM)
    vec = lambda a: pl.BlockSpec(a.shape, lambda b, h, i: (0, 0))
    return pl.pallas_call(
        _attn_kernel,
        grid=grid,
        in_specs=[smem, smem, vec(lq1), vec(lk1), vec(lq2), vec(lk2), vec(og),
                  pl.BlockSpec((None, tq, V_DIM), lambda b, h, i: (b, i, h)),
                  pl.BlockSpec((None, seq, V_DIM), lambda b, h, i: (b, 0, h)),
                  pl.BlockSpec((None, seq, V_DIM), lambda b, h, i: (h, 0, 0)),
                  pl.BlockSpec((None, seq, V_DIM), lambda b, h, i: (b, 0, h))],
        out_specs=pl.BlockSpec((None, tq, V_DIM), lambda b, h, i: (b, i, h)),
        out_shape=jax.ShapeDtypeStruct((bsz, seq, ATT_WIDTH), BF16),
        scratch_shapes=[pltpu.VMEM((2 * tq, 2 * V_DIM), BF16),
                        pltpu.VMEM((2 * tq, tq), F32),
                        pltpu.VMEM((2 * tq, tq), F32),
                        pltpu.VMEM((2 * tq, tq), F32),
                        pltpu.VMEM((2, tq, V_DIM), F32)],
        compiler_params=pltpu.CompilerParams(
            dimension_semantics=("arbitrary", "arbitrary", "arbitrary"), vmem_limit_bytes=VMEM_LIMIT),
        name="attn",
    )(slopes, scal, lq1, lk1, lq2, lk2, og, q, k, kbias, v)


LANES = 128


def _ssm_kernel(u_ref, bmat_ref, cmat_ref, a_ref, d_ref, gw_ref, gb_ref, o_ref, pm_ref, x_ref, st_ref):
    nb, steps, _ = u_ref.shape
    halves = SSM_WIDTH // LANES

    @pl.when(pl.program_id(0) == 0)
    def _():
        st_ref[...] = jnp.zeros_like(st_ref)

    for b in range(nb):
        for hh in range(halves):
            pm_ref[hh, pl.ds(b, steps, stride=nb), :] = u_ref[b, :, hh * LANES:(hh + 1) * LANES]
    u = jnp.concatenate([pm_ref[hh] for hh in range(halves)], axis=1)

    rows = nb * steps
    cuts = (0, rows // 2, rows)
    ub = u.astype(BF16)
    for lo, hi in zip(cuts[:-1], cuts[1:]):
        x_ref[lo:hi, :] = _dot(ub[lo:hi], bmat_ref[...])
    ar = a_ref[:, 0:STATE_WIDTH]
    ai = a_ref[:, STATE_WIDTH:]

    def step(t, carry):
        xr, xi = carry
        r = pl.multiple_of(t * nb, nb)
        nr = ar * xr - ai * xi + x_ref[pl.ds(r, nb), 0:STATE_WIDTH]
        ni = ar * xi + ai * xr + x_ref[pl.ds(r, nb), STATE_WIDTH:]
        x_ref[pl.ds(r, nb), 0:STATE_WIDTH] = nr
        x_ref[pl.ds(r, nb), STATE_WIDTH:] = ni
        return nr, ni

    xr, xi = lax.fori_loop(0, steps, step, (st_ref[:, 0:STATE_WIDTH], st_ref[:, STATE_WIDTH:]))
    st_ref[:, 0:STATE_WIDTH] = xr
    st_ref[:, STATE_WIDTH:] = xi

    y = jnp.concatenate([_dot(x_ref[lo:hi, :].astype(BF16), cmat_ref[...])
                         for lo, hi in zip(cuts[:-1], cuts[1:])], axis=0) + d_ref[...] * u
    y = 0.5 * y * (1.0 + jnp.tanh(math.sqrt(2.0 / math.pi) * (y + 0.044715 * (y * y * y))))
    gate = _dot(y.astype(BF16), gw_ref[...]) + gb_ref[...]
    out = y * _sigmoid(gate)

    for hh in range(halves):
        pm_ref[hh] = out[:, hh * LANES:(hh + 1) * LANES]
    for b in range(nb):
        o_ref[b] = jnp.concatenate(
            [pm_ref[hh, pl.ds(b, steps, stride=nb), :] for hh in range(halves)], axis=1).astype(o_ref.dtype)


def _ssm(u, bmat, cmat, a, d, gw, gb):
    nb, seq, _ = u.shape
    steps = min(SSM_STEPS, seq)
    full = lambda arr: pl.BlockSpec(arr.shape, lambda t: (0, 0))
    tile = pl.BlockSpec((nb, steps, SSM_WIDTH), lambda t: (0, t, 0))
    return pl.pallas_call(
        _ssm_kernel,
        grid=(seq // steps,),
        in_specs=[tile, full(bmat), full(cmat), full(a), full(d), full(gw), full(gb)],
        out_specs=tile,
        out_shape=jax.ShapeDtypeStruct(u.shape, BF16),
        scratch_shapes=[pltpu.VMEM((SSM_WIDTH // LANES, steps * nb, LANES), F32),
                        pltpu.VMEM((steps * nb, 2 * STATE_WIDTH), F32),
                        pltpu.VMEM((nb, 2 * STATE_WIDTH), F32)],
        compiler_params=pltpu.CompilerParams(
            dimension_semantics=("arbitrary",), vmem_limit_bytes=VMEM_LIMIT),
        name="ssm",
    )(u, bmat, cmat, a, d, gw, gb)


def _post_kernel(h_ref, att_ref, cnv_ref, ssm_ref, p_ref, wo_ref, g2_ref, w1_ref, w2_ref,
                 g3_ref, wg_ref, wp_ref, o_ref):
    mixed = jnp.concatenate([att_ref[...], cnv_ref[...], ssm_ref[...]], axis=-1)
    h = h_ref[...] + _dot(mixed, wo_ref[...])

    n2 = _rms(h, g2_ref[...]).astype(BF16)
    d_ff = w1_ref.shape[1]
    for c in range(0, d_ff, FF_CHUNK):
        t = jnp.maximum(_dot(n2, w1_ref[:, c:c + FF_CHUNK]), 0.0)
        h = h + _dot((t * t).astype(BF16), w2_ref[c:c + FF_CHUNK, :])

    gate = _sigmoid(_dot(_rms(h, g3_ref[...]).astype(BF16), wg_ref[...]))
    h = h + gate * _dot(p_ref[...], wp_ref[...])
    o_ref[...] = h


def _post(h, att, cnv, ssm, p, wo, g2, w1, w2, g3, wg, wp):
    bsz, seq, d = h.shape
    tm = min(ROW_TILE, seq)
    grid = (bsz, seq // tm)
    full = lambda a: pl.BlockSpec(a.shape, lambda b, s: (0, 0), pipeline_mode=pl.Buffered(1))
    tok = lambda width: pl.BlockSpec((None, tm, width), lambda b, s: (b, s, 0))
    return pl.pallas_call(
        _post_kernel,
        grid=grid,
        in_specs=[tok(d), tok(ATT_WIDTH), tok(CONV_WIDTH), tok(SSM_WIDTH), tok(p.shape[-1]),
                  full(wo), full(g2), full(w1), full(w2), full(g3), full(wg), full(wp)],
        out_specs=tok(d),
        out_shape=jax.ShapeDtypeStruct(h.shape, F32),
        compiler_params=pltpu.CompilerParams(
            dimension_semantics=("arbitrary", "arbitrary"), vmem_limit_bytes=VMEM_LIMIT),
        name="post",
    )(h, att, cnv, ssm, p, wo, g2, w1, w2, g3, wg, wp)


def _ssm_params(a_re, a_im, log_dt, b_re, b_im, c_re, c_im, nb):
    lam = lax.complex(a_re, a_im)
    dt = jnp.exp(log_dt)[..., None]
    abar = jnp.exp(lam * dt)
    bbar = ((abar - 1.0) / lam)[..., None] * lax.complex(b_re, b_im)
    eye = jnp.eye(SSM_GROUPS, dtype=F32)
    nl = a_re.shape[0]

    def in_blockdiag(m):
        m = jnp.swapaxes(m, 2, 3)[:, :, :, None, :] * eye[None, :, None, :, None]
        return m.reshape(nl, SSM_WIDTH, STATE_WIDTH)

    def out_blockdiag(m):
        m = jnp.swapaxes(m, 2, 3)[:, :, :, None, :] * eye[None, :, None, :, None]
        return m.reshape(nl, STATE_WIDTH, SSM_WIDTH)

    bmat = jnp.concatenate([in_blockdiag(jnp.real(bbar)), in_blockdiag(jnp.imag(bbar))], axis=2)
    cmat = jnp.concatenate([out_blockdiag(c_re), -out_blockdiag(c_im)], axis=1)
    a = jnp.concatenate([jnp.real(abar).reshape(nl, 1, STATE_WIDTH),
                         jnp.imag(abar).reshape(nl, 1, STATE_WIDTH)], axis=2)
    a = jnp.broadcast_to(a, (nl, nb, 2 * STATE_WIDTH))
    return bmat.astype(BF16), cmat.astype(BF16), a


def kernel(x, p, mix_norm, w_in, q_norm, k_norm, lam_q1, lam_k1, lam_q2, lam_k2, att_out_norm, conv_w, conv_b, conv_ln_g, conv_ln_b, ssm_a_re, ssm_a_im, ssm_log_dt, ssm_b_re, ssm_b_im, ssm_c_re, ssm_c_im, ssm_d, ssm_glu_w, ssm_glu_b, w_out, mlp_norm, w_ff1, w_ff2, ple_norm, w_ple_gate, w_ple_proj):
    bsz, seq, d_model = x.shape
    depth = w_in.shape[0]
    assert bsz % 8 == 0 and seq % CHUNK == 0

    bmat, cmat, abar = _ssm_params(ssm_a_re, ssm_a_im, ssm_log_dt, ssm_b_re, ssm_b_im,
                                   ssm_c_re, ssm_c_im, bsz)
    group = jnp.arange(QK_WIDTH) // HEAD_DIM
    ones = (group[:, None] == group[None, :]).astype(BF16)
    log2e = math.log2(math.e)
    slopes = 2.0 ** (-8.0 * jnp.arange(1, N_HEADS + 1, dtype=F32) / N_HEADS)
    kbias = _key_bias(slopes, seq)
    slopes = slopes * log2e
    scale = HEAD_DIM ** -0.5 * log2e
    conv_w_rep = jnp.repeat(conv_w, 8, axis=1)
    p_bf = p.astype(BF16)
    row = lambda a: a.reshape(1, -1)

    h = x
    for i in range(depth):
        lam_init = 0.8 - 0.6 * math.exp(-0.3 * i)
        q, k, v, cnv, zs = _pre(
            h, row(mix_norm[i]), w_in[i].astype(BF16),
            row(jnp.tile(q_norm[i], 2 * N_HEADS) * scale), row(jnp.tile(k_norm[i], 2 * N_HEADS)), ones,
            conv_w_rep[i], row(conv_b[i]), row(conv_ln_g[i]), row(conv_ln_b[i]))
        att = _attn(q, k, kbias, v, slopes, jnp.array([lam_init, 1.0 - lam_init], F32),
                    row(lam_q1[i]), row(lam_k1[i]), row(lam_q2[i]), row(lam_k2[i]), row(att_out_norm[i]))
        ssm = _ssm(zs, bmat[i], cmat[i], abar[i],
                   row(ssm_d[i]), ssm_glu_w[i].astype(BF16), row(ssm_glu_b[i]))
        h = _post(h, att, cnv, ssm, p_bf[i],
                  w_out[i].astype(BF16), row(mlp_norm[i]), w_ff1[i].astype(BF16), w_ff2[i].astype(BF16),
                  row(ple_norm[i]), w_ple_gate[i].astype(BF16), w_ple_proj[i].astype(BF16))
    return h
```

```python
import functools
import math

import numpy as np
import jax
import jax.numpy as jnp
from jax import lax
from jax.experimental import pallas as pl
from jax.experimental.pallas import tpu as pltpu

F32 = jnp.float32
BF16 = jnp.bfloat16

EPS = 1e-6
CHUNK = 64
HEAD_DIM = 64
V_DIM = 2 * HEAD_DIM
N_HEADS = 4
QK_WIDTH = N_HEADS * 2 * HEAD_DIM
ATT_WIDTH = N_HEADS * V_DIM
CONV_WIDTH = 256
CONV_KERNEL = 31
CONV_HALO = 32
SSM_WIDTH = 256
SSM_GROUP = 16
SSM_GROUPS = SSM_WIDTH // SSM_GROUP
SSM_STATE = 64
STATE_WIDTH = SSM_GROUPS * SSM_STATE

ROW_TILE = 512
CONV_ROWS = 64
SSM_STEPS = 64
FF_CHUNK = 1024
VMEM_LIMIT = 56 * 1024 * 1024


def _rms(x, g):
    return x * lax.rsqrt(jnp.mean(x * x, axis=-1, keepdims=True) + EPS) * g


def _dot(a, b):
    return jnp.dot(a, b, preferred_element_type=F32)


def _sigmoid(x):
    return 1.0 / (1.0 + jnp.exp(-x))


def _pre_kernel(h_ref, g_ref, w_ref, qg_ref, kg_ref, ones_ref, cw_ref, cb_ref, cg_ref, cbeta_ref,
                q_ref, k_ref, v_ref, c_ref, s_ref, buf_ref, sh_ref):
    tm = h_ref.shape[0]
    n = _rms(h_ref[...], g_ref[...]).astype(BF16)
    col = {"q": 0, "k": QK_WIDTH, "v": 2 * QK_WIDTH, "c": 2 * QK_WIDTH + ATT_WIDTH,
           "s": 2 * QK_WIDTH + ATT_WIDTH + 2 * CONV_WIDTH}

    def proj(name, width):
        return _dot(n, w_ref[:, col[name]:col[name] + width])

    first = pl.program_id(1) == 0

    @pl.when(first)
    def _():
        buf_ref[0:CONV_HALO, :] = jnp.zeros((CONV_HALO, CONV_WIDTH), F32)

    @pl.when(jnp.logical_not(first))
    def _():
        buf_ref[0:CONV_HALO, :] = buf_ref[tm:tm + CONV_HALO, :]

    zc = proj("c", 2 * CONV_WIDTH)
    buf_ref[CONV_HALO:, :] = zc[:, :CONV_WIDTH] * _sigmoid(zc[:, CONV_WIDTH:])
    span = sh_ref.shape[1]
    for r in range(1, 8):
        sh_ref[r - 1] = buf_ref[r:r + span, :]
    lead = CONV_HALO - (CONV_KERNEL - 1)
    groups = CONV_ROWS // 8

    def conv_rows(rb):
        acc = jnp.zeros((groups, 8, CONV_WIDTH), F32)
        for j in range(CONV_KERNEL):
            m8, r = divmod(lead + j, 8)
            if r == 0:
                src = buf_ref[rb + 8 * m8:rb + 8 * m8 + CONV_ROWS, :]
            else:
                src = sh_ref[r - 1, rb + 8 * m8:rb + 8 * m8 + CONV_ROWS, :]
            acc = acc + cw_ref[8 * j:8 * j + 8, :][None] * src.reshape(groups, 8, CONV_WIDTH)
        y = acc.reshape(CONV_ROWS, CONV_WIDTH) + cb_ref[...]
        mu = jnp.mean(y, axis=-1, keepdims=True)
        d = y - mu
        var = jnp.mean(d * d, axis=-1, keepdims=True)
        yn = d * lax.rsqrt(var + EPS) * cg_ref[...] + cbeta_ref[...]
        c_ref[rb:rb + CONV_ROWS, :] = (yn * _sigmoid(yn)).astype(c_ref.dtype)

    def head_norm(zz, gain):
        sq = (zz * zz).astype(BF16)
        return zz * lax.rsqrt(_dot(sq, ones_ref[...]) * (1.0 / HEAD_DIM) + EPS) * gain

    q_ref[...] = head_norm(proj("q", QK_WIDTH), qg_ref[...]).astype(BF16)
    k_ref[...] = head_norm(proj("k", QK_WIDTH), kg_ref[...]).astype(BF16)
    v_ref[...] = proj("v", ATT_WIDTH).T.astype(BF16)
    s_ref[...] = proj("s", SSM_WIDTH)

    for rb in range(0, tm, CONV_ROWS):
        conv_rows(rb)


def _pre(h, g, w, qg, kg, ones, cw, cb, cg, cbeta):
    bsz, seq, d = h.shape
    tm = min(ROW_TILE, seq)
    grid = (bsz, seq // tm)
    full = lambda a: pl.BlockSpec(a.shape, lambda b, s: (0,) * a.ndim)
    tok = lambda width: pl.BlockSpec((None, tm, width), lambda b, s: (b, s, 0))
    return pl.pallas_call(
        _pre_kernel,
        grid=grid,
        in_specs=[tok(d)] + [full(a) for a in (g, w, qg, kg, ones, cw, cb, cg, cbeta)],
        out_specs=[tok(QK_WIDTH), tok(QK_WIDTH),
                   pl.BlockSpec((None, None, ATT_WIDTH, tm), lambda b, s: (b, s, 0, 0)),
                   tok(CONV_WIDTH), tok(SSM_WIDTH)],
        out_shape=[jax.ShapeDtypeStruct((bsz, seq, QK_WIDTH), BF16),
                   jax.ShapeDtypeStruct((bsz, seq, QK_WIDTH), BF16),
                   jax.ShapeDtypeStruct((bsz, seq // tm, ATT_WIDTH, tm), BF16),
                   jax.ShapeDtypeStruct((bsz, seq, CONV_WIDTH), BF16),
                   jax.ShapeDtypeStruct((bsz, seq, SSM_WIDTH), F32)],
        scratch_shapes=[pltpu.VMEM((tm + CONV_HALO, CONV_WIDTH), F32),
                        pltpu.VMEM((7, tm + CONV_HALO - 8, CONV_WIDTH), F32)],
        compiler_params=pltpu.CompilerParams(
            dimension_semantics=("arbitrary", "arbitrary"), vmem_limit_bytes=VMEM_LIMIT),
        name="pre",
    )(h, g, w, qg, kg, ones, cw, cb, cg, cbeta)


def _attn_kernel(slopes_ref, scal_ref, lq1_ref, lk1_ref, lq2_ref, lk2_ref, og_ref,
                 q_ref, k_ref, kb_ref, vt_ref, o_ref, qq_ref, sd_ref, sa_ref, sb_ref, acc_ref):
    tq = q_ref.shape[0]
    head = pl.program_id(1)
    i = pl.program_id(2)
    slope = slopes_ref[head]

    q = q_ref[...]
    lane = lax.broadcasted_iota(jnp.int32, q.shape, 1)
    zero = jnp.zeros_like(q)
    pieces = jnp.zeros(q.shape, F32)
    for col_, piece in enumerate(LOG2E_PIECES + LOG2E_PIECES):
        pieces = jnp.where(lane == col_, piece, pieces)
    pieces = pieces.astype(q.dtype)
    qq_ref[0:tq, :] = jnp.concatenate([jnp.where(lane < HEAD_DIM, q, zero), pieces], axis=1)
    qq_ref[tq:, :] = jnp.concatenate([jnp.where(lane >= HEAD_DIM, q, zero), pieces], axis=1)
    acc_ref[...] = jnp.zeros_like(acc_ref)

    def scores(t, s_ref):
        start = pl.multiple_of(t * tq, tq)
        kt = jnp.concatenate([k_ref[pl.ds(start, tq), :], kb_ref[pl.ds(start, tq), :]], axis=1)
        s_ref[...] = lax.dot_general(kt, qq_ref[...], (((1,), (1,)), ((), ())), preferred_element_type=F32)

    def update(t, s_ref, carry, fix):
        m_old, l_old = carry
        s = s_ref[...]
        if fix is not None:
            s = s + fix
        m_new = jnp.maximum(m_old, jnp.max(s, axis=0, keepdims=True))
        alpha = jnp.exp2(m_old - m_new)
        p = jnp.exp2(s - m_new)
        acc_ref[...] = alpha * acc_ref[...] + _dot(vt_ref[t], p.astype(BF16))
        return m_new, alpha * l_old + jnp.sum(p, axis=0, keepdims=True)

    scores(i, sd_ref)
    scores(0, sa_ref)
    key = lax.broadcasted_iota(jnp.int32, (tq, tq), 0)
    qry = lax.broadcasted_iota(jnp.int32, (tq, tq), 1)
    fix = jnp.minimum((2.0 * slope) * (qry - key).astype(F32), 0.0)
    fix = jnp.where((key // CHUNK) <= (qry // CHUNK), fix, -1e30)
    neg = jnp.full((1, 2 * tq), -1e30, F32)
    nil = jnp.zeros((1, 2 * tq), F32)
    carry = update(i, sd_ref, (neg, nil), jnp.concatenate([fix, fix], axis=1))

    def pair(u, carry):
        scores(2 * u + 1, sb_ref)
        carry = update(2 * u, sa_ref, carry, None)
        scores(2 * u + 2, sa_ref)
        return update(2 * u + 1, sb_ref, carry, None)

    carry = lax.fori_loop(0, i // 2, pair, carry)
    _, l = lax.cond(i % 2 == 1, lambda cr: update(i - 1, sa_ref, cr, None), lambda cr: cr, carry)

    lam_init = scal_ref[0]
    lam = (jnp.exp(jnp.sum(lq1_ref[...] * lk1_ref[...], axis=-1, keepdims=True))
           - jnp.exp(jnp.sum(lq2_ref[...] * lk2_ref[...], axis=-1, keepdims=True)) + lam_init)
    inv = 1.0 / l
    o = acc_ref[:, 0:tq] * inv[:, 0:tq] - lam * (acc_ref[:, tq:] * inv[:, tq:])
    o = o * lax.rsqrt(jnp.mean(o * o, axis=0, keepdims=True) + EPS) * og_ref[...] * scal_ref[1]
    o_ref[...] = o.T.astype(o_ref.dtype)


def _bf16_pieces(x, n):
    out = []
    for _ in range(n):
        piece = float(np.asarray(x, np.float32).astype(BF16).astype(np.float32))
        out.append(piece)
        x -= piece
    return tuple(out)


LOG2E_PIECES = _bf16_pieces(math.log2(math.e), 3)
BIAS_COLS = 2 * len(LOG2E_PIECES)


def _key_bias(slopes, seq):
    kpos = jnp.arange(seq, dtype=jnp.int32)
    lo = (kpos % CHUNK).astype(F32)
    hi = (kpos - kpos % CHUNK).astype(F32)
    n = len(LOG2E_PIECES)
    cols = jnp.stack([hi] * n + [lo] * n, axis=-1)[None] * slopes[:, None, None]
    return jnp.pad(cols.astype(BF16), ((0, 0), (0, 0), (0, V_DIM - BIAS_COLS)))


def _attn(q, k, kbias, vt, slopes, scal, lq1, lk1, lq2, lk2, og):
    bsz, seq, _ = q.shape
    n_tiles, tq = vt.shape[1], vt.shape[3]
    grid = (bsz, N_HEADS, n_tiles)
    smem = pl.BlockSpec(memory_space=pltpu.SMEM)
    vec = lambda a: pl.BlockSpec(a.shape, lambda b, h, i: (0, 0))
    return pl.pallas_call(
        _attn_kernel,
        grid=grid,
        in_specs=[smem, smem, vec(lq1), vec(lk1), vec(lq2), vec(lk2), vec(og),
                  pl.BlockSpec((None, tq, V_DIM), lambda b, h, i: (b, i, h)),
                  pl.BlockSpec((None, seq, V_DIM), lambda b, h, i: (b, 0, h)),
                  pl.BlockSpec((None, seq, V_DIM), lambda b, h, i: (h, 0, 0)),
                  pl.BlockSpec((None, n_tiles, V_DIM, tq), lambda b, h, i: (b, 0, h, 0))],
        out_specs=pl.BlockSpec((None, tq, V_DIM), lambda b, h, i: (b, i, h)),
        out_shape=jax.ShapeDtypeStruct((bsz, seq, ATT_WIDTH), BF16),
        scratch_shapes=[pltpu.VMEM((2 * tq, 2 * V_DIM), BF16),
                        pltpu.VMEM((tq, 2 * tq), F32),
                        pltpu.VMEM((tq, 2 * tq), F32),
                        pltpu.VMEM((tq, 2 * tq), F32),
                        pltpu.VMEM((V_DIM, 2 * tq), F32)],
        compiler_params=pltpu.CompilerParams(
            dimension_semantics=("arbitrary", "arbitrary", "arbitrary"), vmem_limit_bytes=VMEM_LIMIT),
        name="attn",
    )(slopes, scal, lq1, lk1, lq2, lk2, og, q, k, kbias, vt)


LANES = 128


def _ssm_kernel(u_ref, bmat_ref, cmat_ref, a_ref, d_ref, gw_ref, gb_ref, o_ref, pm_ref, x_ref, st_ref):
    nb, steps, _ = u_ref.shape
    halves = SSM_WIDTH // LANES

    @pl.when(pl.program_id(0) == 0)
    def _():
        st_ref[...] = jnp.zeros_like(st_ref)

    for b in range(nb):
        for hh in range(halves):
            pm_ref[hh, pl.ds(b, steps, stride=nb), :] = u_ref[b, :, hh * LANES:(hh + 1) * LANES]
    u = jnp.concatenate([pm_ref[hh] for hh in range(halves)], axis=1)

    rows = nb * steps
    cuts = (0, rows // 2, rows)
    ub = u.astype(BF16)
    for lo, hi in zip(cuts[:-1], cuts[1:]):
        x_ref[lo:hi, :] = _dot(ub[lo:hi], bmat_ref[...])
    ar = a_ref[:, 0:STATE_WIDTH]
    ai = a_ref[:, STATE_WIDTH:]

    def step(t, carry):
        xr, xi = carry
        r = pl.multiple_of(t * nb, nb)
        nr = ar * xr - ai * xi + x_ref[pl.ds(r, nb), 0:STATE_WIDTH]
        ni = ar * xi + ai * xr + x_ref[pl.ds(r, nb), STATE_WIDTH:]
        x_ref[pl.ds(r, nb), 0:STATE_WIDTH] = nr
        x_ref[pl.ds(r, nb), STATE_WIDTH:] = ni
        return nr, ni

    xr, xi = lax.fori_loop(0, steps, step, (st_ref[:, 0:STATE_WIDTH], st_ref[:, STATE_WIDTH:]))
    st_ref[:, 0:STATE_WIDTH] = xr
    st_ref[:, STATE_WIDTH:] = xi

    y = jnp.concatenate([_dot(x_ref[lo:hi, :].astype(BF16), cmat_ref[...])
                         for lo, hi in zip(cuts[:-1], cuts[1:])], axis=0) + d_ref[...] * u
    y = 0.5 * y * (1.0 + jnp.tanh(math.sqrt(2.0 / math.pi) * (y + 0.044715 * (y * y * y))))
    gate = _dot(y.astype(BF16), gw_ref[...]) + gb_ref[...]
    out = y * _sigmoid(gate)

    for hh in range(halves):
        pm_ref[hh] = out[:, hh * LANES:(hh + 1) * LANES]
    for b in range(nb):
        o_ref[b] = jnp.concatenate(
            [pm_ref[hh, pl.ds(b, steps, stride=nb), :] for hh in range(halves)], axis=1).astype(o_ref.dtype)


def _ssm(u, bmat, cmat, a, d, gw, gb):
    nb, seq, _ = u.shape
    steps = min(SSM_STEPS, seq)
    full = lambda arr: pl.BlockSpec(arr.shape, lambda t: (0, 0))
    tile = pl.BlockSpec((nb, steps, SSM_WIDTH), lambda t: (0, t, 0))
    return pl.pallas_call(
        _ssm_kernel,
        grid=(seq // steps,),
        in_specs=[tile, full(bmat), full(cmat), full(a), full(d), full(gw), full(gb)],
        out_specs=tile,
        out_shape=jax.ShapeDtypeStruct(u.shape, BF16),
        scratch_shapes=[pltpu.VMEM((SSM_WIDTH // LANES, steps * nb, LANES), F32),
                        pltpu.VMEM((steps * nb, 2 * STATE_WIDTH), F32),
                        pltpu.VMEM((nb, 2 * STATE_WIDTH), F32)],
        compiler_params=pltpu.CompilerParams(
            dimension_semantics=("arbitrary",), vmem_limit_bytes=VMEM_LIMIT),
        name="ssm",
    )(u, bmat, cmat, a, d, gw, gb)


def _post_kernel(h_ref, att_ref, cnv_ref, ssm_ref, p_ref, wo_ref, g2_ref, w1_ref, w2_ref,
                 g3_ref, wg_ref, wp_ref, o_ref):
    mixed = jnp.concatenate([att_ref[...], cnv_ref[...], ssm_ref[...]], axis=-1)
    h = h_ref[...] + _dot(mixed, wo_ref[...])

    n2 = _rms(h, g2_ref[...]).astype(BF16)
    d_ff = w1_ref.shape[1]
    for c in range(0, d_ff, FF_CHUNK):
        t = jnp.maximum(_dot(n2, w1_ref[:, c:c + FF_CHUNK]), 0.0)
        h = h + _dot((t * t).astype(BF16), w2_ref[c:c + FF_CHUNK, :])

    gate = _sigmoid(_dot(_rms(h, g3_ref[...]).astype(BF16), wg_ref[...]))
    h = h + gate * _dot(p_ref[...], wp_ref[...])
    o_ref[...] = h


def _post(h, att, cnv, ssm, p, wo, g2, w1, w2, g3, wg, wp):
    bsz, seq, d = h.shape
    tm = min(ROW_TILE, seq)
    grid = (bsz, seq // tm)
    full = lambda a: pl.BlockSpec(a.shape, lambda b, s: (0, 0), pipeline_mode=pl.Buffered(1))
    tok = lambda width: pl.BlockSpec((None, tm, width), lambda b, s: (b, s, 0))
    return pl.pallas_call(
        _post_kernel,
        grid=grid,
        in_specs=[tok(d), tok(ATT_WIDTH), tok(CONV_WIDTH), tok(SSM_WIDTH), tok(p.shape[-1]),
                  full(wo), full(g2), full(w1), full(w2), full(g3), full(wg), full(wp)],
        out_specs=tok(d),
        out_shape=jax.ShapeDtypeStruct(h.shape, F32),
        compiler_params=pltpu.CompilerParams(
            dimension_semantics=("arbitrary", "arbitrary"), vmem_limit_bytes=VMEM_LIMIT),
        name="post",
    )(h, att, cnv, ssm, p, wo, g2, w1, w2, g3, wg, wp)


def _ssm_params(a_re, a_im, log_dt, b_re, b_im, c_re, c_im, nb):
    lam = lax.complex(a_re, a_im)
    dt = jnp.exp(log_dt)[..., None]
    abar = jnp.exp(lam * dt)
    bbar = ((abar - 1.0) / lam)[..., None] * lax.complex(b_re, b_im)
    eye = jnp.eye(SSM_GROUPS, dtype=F32)
    nl = a_re.shape[0]

    def in_blockdiag(m):
        m = jnp.swapaxes(m, 2, 3)[:, :, :, None, :] * eye[None, :, None, :, None]
        return m.reshape(nl, SSM_WIDTH, STATE_WIDTH)

    def out_blockdiag(m):
        m = jnp.swapaxes(m, 2, 3)[:, :, :, None, :] * eye[None, :, None, :, None]
        return m.reshape(nl, STATE_WIDTH, SSM_WIDTH)

    bmat = jnp.concatenate([in_blockdiag(jnp.real(bbar)), in_blockdiag(jnp.imag(bbar))], axis=2)
    cmat = jnp.concatenate([out_blockdiag(c_re), -out_blockdiag(c_im)], axis=1)
    a = jnp.concatenate([jnp.real(abar).reshape(nl, 1, STATE_WIDTH),
                         jnp.imag(abar).reshape(nl, 1, STATE_WIDTH)], axis=2)
    a = jnp.broadcast_to(a, (nl, nb, 2 * STATE_WIDTH))
    return bmat.astype(BF16), cmat.astype(BF16), a


def kernel(x, p, mix_norm, w_in, q_norm, k_norm, lam_q1, lam_k1, lam_q2, lam_k2, att_out_norm, conv_w, conv_b, conv_ln_g, conv_ln_b, ssm_a_re, ssm_a_im, ssm_log_dt, ssm_b_re, ssm_b_im, ssm_c_re, ssm_c_im, ssm_d, ssm_glu_w, ssm_glu_b, w_out, mlp_norm, w_ff1, w_ff2, ple_norm, w_ple_gate, w_ple_proj):
    bsz, seq, d_model = x.shape
    depth = w_in.shape[0]
    assert bsz % 8 == 0 and seq % CHUNK == 0

    bmat, cmat, abar = _ssm_params(ssm_a_re, ssm_a_im, ssm_log_dt, ssm_b_re, ssm_b_im,
                                   ssm_c_re, ssm_c_im, bsz)
    group = jnp.arange(QK_WIDTH) // HEAD_DIM
    ones = (group[:, None] == group[None, :]).astype(BF16)
    log2e = math.log2(math.e)
    slopes = 2.0 ** (-8.0 * jnp.arange(1, N_HEADS + 1, dtype=F32) / N_HEADS)
    kbias = _key_bias(slopes, seq)
    slopes = slopes * log2e
    scale = HEAD_DIM ** -0.5 * log2e
    conv_w_rep = jnp.repeat(conv_w, 8, axis=1)
    p_bf = p.astype(BF16)
    row = lambda a: a.reshape(1, -1)

    h = x
    for i in range(depth):
        lam_init = 0.8 - 0.6 * math.exp(-0.3 * i)
        q, k, v, cnv, zs = _pre(
            h, row(mix_norm[i]), w_in[i].astype(BF16),
            row(jnp.tile(q_norm[i], 2 * N_HEADS) * scale), row(jnp.tile(k_norm[i], 2 * N_HEADS)), ones,
            conv_w_rep[i], row(conv_b[i]), row(conv_ln_g[i]), row(conv_ln_b[i]))
        att = _attn(q, k, kbias, v, slopes, jnp.array([lam_init, 1.0 - lam_init], F32),
                    row(lam_q1[i]), row(lam_k1[i]), row(lam_q2[i]), row(lam_k2[i]), att_out_norm[i].reshape(-1, 1))
        ssm = _ssm(zs, bmat[i], cmat[i], abar[i],
                   row(ssm_d[i]), ssm_glu_w[i].astype(BF16), row(ssm_glu_b[i]))
        h = _post(h, att, cnv, ssm, p_bf[i],
                  w_out[i].astype(BF16), row(mlp_norm[i]), w_ff1[i].astype(BF16), w_ff2[i].astype(BF16),
                  row(ple_norm[i]), w_ple_gate[i].astype(BF16), w_ple_proj[i].astype(BF16))
    return h
```
